```python
import math
import jax, jax.numpy as jnp
from jax import lax
import numpy as np

D_MODEL = 2048
BATCH = 4
SEQ = 8192
DEPTH = 1
DEC_BATCH = 32
DEC_SEQ = 32
PAST_LEN = 4096

CHUNK = 64
D_MIX = D_MODEL
D_ATT = D_MIX // 2
D_SSM = D_MIX - D_ATT
HEAD_DIM = 128
N_HEADS = D_ATT // HEAD_DIM
SSM_GROUP = 16
N_GROUPS = D_SSM // SSM_GROUP
STATE = 64
D_PLE = 256
Q_BLOCK = 128
EPS = 1e-6
NEG_INF = -1e30

OFF_K = D_ATT
OFF_V = 2 * D_ATT
OFF_F = 3 * D_ATT
OFF_GA = 3 * D_ATT + N_HEADS
OFF_U = 4 * D_ATT + N_HEADS
OFF_GS = OFF_U + D_SSM
D_IN = OFF_GS + D_SSM

kernel_name = 'fox_s5_hymba_stream_step'


def _rmsnorm(x, g):
    xf = x.astype(jnp.float32)
    y = xf * lax.rsqrt(jnp.mean(xf * xf, axis=-1, keepdims=True) + EPS)
    return (y * g.astype(jnp.float32)).astype(x.dtype)


def _fox_block(q, cq, qpos, k, v, ck_t, kpos):
    s = jnp.einsum('bthd,bshd->bhts', q, k, preferred_element_type=jnp.float32) * (HEAD_DIM ** -0.5)
    bias = jnp.transpose(cq.astype(jnp.float32), (0, 2, 1))[..., :, None] - ck_t[:, :, None, :]
    mask = kpos[None, :] <= qpos[:, None]
    s = jnp.where(mask, s + bias, NEG_INF)
    pr = jax.nn.softmax(s, axis=-1)
    return jnp.einsum('bhts,bshd->bthd', pr.astype(v.dtype), v)


def _fox_attention(q, cq, qpos, k, v, ck, kpos):
    b, t = q.shape[0], q.shape[1]
    ck_t = jnp.transpose(ck.astype(jnp.float32), (0, 2, 1))
    if t <= Q_BLOCK:
        return _fox_block(q, cq, qpos, k, v, ck_t, kpos)
    nb = t // Q_BLOCK
    qb = q.reshape(b, nb, Q_BLOCK, N_HEADS, HEAD_DIM).transpose(1, 0, 2, 3, 4)
    cqb = cq.reshape(b, nb, Q_BLOCK, N_HEADS).transpose(1, 0, 2, 3)
    pb = qpos.reshape(nb, Q_BLOCK)
    ob = lax.map(lambda a: _fox_block(a[0], a[1], a[2], k, v, ck_t, kpos), (qb, cqb, pb))
    return ob.transpose(1, 0, 2, 3, 4).reshape(b, t, N_HEADS, HEAD_DIM)


def _ssm_combine(left, right):
    a1, b1 = left
    a2, b2 = right
    return a1 * a2, a2 * b1 + b2


def _s5(u, h0_re, h0_im, a_re, a_im, log_dt, b_re, b_im, c_re, c_im, d_skip, w_glu):
    f32 = jnp.float32
    b, t, _ = u.shape
    ug = u.astype(f32).reshape(b, t, N_GROUPS, SSM_GROUP)
    lam = lax.complex(a_re.astype(f32), a_im.astype(f32))
    dt = jnp.exp(log_dt.astype(f32))[:, None]
    a_bar = jnp.exp(lam * dt)
    b_bar = ((a_bar - 1.0) / lam)[:, :, None] * lax.complex(b_re.astype(f32), b_im.astype(f32))
    bu = jnp.einsum('gpc,btgc->btgp', b_bar, ug.astype(jnp.complex64))
    if h0_re is not None:
        h0 = lax.complex(h0_re.astype(f32), h0_im.astype(f32))
        bu = bu.at[:, 0].add(a_bar * h0)
    a_seq = jnp.broadcast_to(a_bar, (1, t, N_GROUPS, STATE))
    _, hs = lax.associative_scan(_ssm_combine, (a_seq, bu), axis=1)
    c = lax.complex(c_re.astype(f32), c_im.astype(f32))
    y = jnp.real(jnp.einsum('gcp,btgp->btgc', c, hs)) + d_skip.astype(f32).reshape(N_GROUPS, SSM_GROUP) * ug
    y = jax.nn.gelu(y.reshape(b, t, D_SSM))
    y = y * jax.nn.sigmoid(y @ w_glu.astype(f32))
    h_last = hs[:, -1]
    return y.astype(u.dtype), jnp.real(h_last), jnp.imag(h_last)


def _layer(h, pe, past, g_in, w_in, b_f, a_re, a_im, log_dt, b_re, b_im, c_re, c_im,
           d_skip, w_glu, w_out, w_pe, g_pe, w_pg):
    bsz, t, _ = h.shape
    n = _rmsnorm(h, g_in)
    z = n @ w_in
    q, k, v, fl, ga, u, gs = jnp.split(z, [OFF_K, OFF_V, OFF_F, OFF_GA, OFF_U, OFF_GS], axis=-1)
    q = q.reshape(bsz, t, N_HEADS, HEAD_DIM)
    k = k.reshape(bsz, t, N_HEADS, HEAD_DIM)
    v = v.reshape(bsz, t, N_HEADS, HEAD_DIM)
    logf = jax.nn.log_sigmoid((fl + b_f).astype(jnp.float32))
    if past is None:
        k_all, v_all, logf_all, h0_re, h0_im = k, v, logf, None, None
        n_past = 0
    else:
        k_past, v_past, logf_past, h0_re, h0_im = past
        n_past = k_past.shape[1]
        k_all = jnp.concatenate([k_past.astype(k.dtype), k], axis=1)
        v_all = jnp.concatenate([v_past.astype(v.dtype), v], axis=1)
        logf_all = jnp.concatenate([logf_past.astype(jnp.float32), logf], axis=1)
    c_all = jnp.cumsum(logf_all, axis=1)
    kpos = jnp.arange(n_past + t)
    qpos = n_past + jnp.arange(t)
    att = _fox_attention(q, c_all[:, n_past:], qpos, k_all, v_all, c_all, kpos).reshape(bsz, t, D_ATT)
    ssm, s_re, s_im = _s5(u, h0_re, h0_im, a_re, a_im, log_dt, b_re, b_im, c_re, c_im, d_skip, w_glu)
    mixed = jnp.concatenate([att * jax.nn.silu(ga), ssm * jax.nn.silu(gs)], axis=-1)
    h = h + mixed @ w_out
    e = _rmsnorm(pe @ w_pe, g_pe)
    h = h + e * jax.nn.sigmoid(h @ w_pg)
    return h, k, v, logf, s_re, s_im


def _trunk(x, p, past, g_in, w_in, b_f, a_re, a_im, log_dt, b_re, b_im, c_re, c_im,
           d_skip, w_glu, w_out, w_pe, g_pe, w_pg, g_final):
    h = x
    ks, vs, lfs, srs, sis = [], [], [], [], []
    for i in range(DEPTH):
        lp = None if past is None else (past[0][i], past[1][i], past[2][i], past[3][i], past[4][i])
        h, k, v, lf, sr, si = _layer(h, p[i], lp, g_in[i], w_in[i], b_f[i], a_re[i], a_im[i], log_dt[i],
                                     b_re[i], b_im[i], c_re[i], c_im[i], d_skip[i], w_glu[i], w_out[i],
                                     w_pe[i], g_pe[i], w_pg[i])
        ks.append(k)
        vs.append(v)
        lfs.append(lf)
        srs.append(sr)
        sis.append(si)
    y = _rmsnorm(h, g_final)
    return y, jnp.stack(ks), jnp.stack(vs), jnp.stack(lfs), jnp.stack(srs), jnp.stack(sis)


def setup_inputs(seed: int = 0) -> dict:
    key = jax.random.key(seed)
    ks = jax.random.split(key, 32)
    nrm = jax.random.normal
    f32 = jnp.float32
    x_prompt = nrm(ks[0], (BATCH, SEQ, D_MODEL), f32)
    x_sample = nrm(ks[1], (DEC_BATCH, DEC_SEQ, D_MODEL), f32)
    p_prompt = nrm(ks[2], (DEPTH, BATCH, SEQ, D_PLE), f32)
    p_sample = nrm(ks[3], (DEPTH, DEC_BATCH, DEC_SEQ, D_PLE), f32)
    cache_k = nrm(ks[4], (DEPTH, DEC_BATCH, PAST_LEN, N_HEADS, HEAD_DIM), f32)
    cache_v = nrm(ks[5], (DEPTH, DEC_BATCH, PAST_LEN, N_HEADS, HEAD_DIM), f32)
    cache_logf = jax.nn.log_sigmoid(2.0 + nrm(ks[6], (DEPTH, DEC_BATCH, PAST_LEN, N_HEADS), f32))
    state_ssm_re = 0.1 * nrm(ks[7], (DEPTH, DEC_BATCH, N_GROUPS, STATE), f32)
    state_ssm_im = 0.1 * nrm(ks[8], (DEPTH, DEC_BATCH, N_GROUPS, STATE), f32)
    g_in = 1.0 + 0.02 * nrm(ks[9], (DEPTH, D_MODEL), f32)
    w_in = nrm(ks[10], (DEPTH, D_MODEL, D_IN), f32) * D_MODEL ** -0.5
    b_f = 2.0 + 0.1 * nrm(ks[11], (DEPTH, N_HEADS), f32)
    a_re = -0.5 + 0.01 * nrm(ks[12], (DEPTH, N_GROUPS, STATE), f32)
    a_im = math.pi * jnp.arange(STATE, dtype=f32) + 0.01 * nrm(ks[13], (DEPTH, N_GROUPS, STATE), f32)
    log_dt = jax.random.uniform(ks[14], (DEPTH, N_GROUPS), f32, math.log(1e-3), math.log(1e-1))
    b_re = nrm(ks[15], (DEPTH, N_GROUPS, STATE, SSM_GROUP), f32) * (2 * SSM_GROUP) ** -0.5
    b_im = nrm(ks[16], (DEPTH, N_GROUPS, STATE, SSM_GROUP), f32) * (2 * SSM_GROUP) ** -0.5
    c_re = nrm(ks[17], (DEPTH, N_GROUPS, SSM_GROUP, STATE), f32) * STATE ** -0.5
    c_im = nrm(ks[18], (DEPTH, N_GROUPS, SSM_GROUP, STATE), f32) * STATE ** -0.5
    d_skip = nrm(ks[19], (DEPTH, D_SSM), f32)
    w_glu = nrm(ks[20], (DEPTH, D_SSM, D_SSM), f32) * D_SSM ** -0.5
    w_out = nrm(ks[21], (DEPTH, D_MIX, D_MODEL), f32) * D_MIX ** -0.5
    w_pe = nrm(ks[22], (DEPTH, D_PLE, D_MODEL), f32) * D_PLE ** -0.5
    g_pe = 1.0 + 0.02 * nrm(ks[23], (DEPTH, D_MODEL), f32)
    w_pg = nrm(ks[24], (DEPTH, D_MODEL, D_MODEL), f32) * D_MODEL ** -0.5
    g_final = 1.0 + 0.02 * nrm(ks[25], (D_MODEL,), f32)
    return {'x_prompt': x_prompt, 'x_sample': x_sample, 'p_prompt': p_prompt, 'p_sample': p_sample,
            'cache_k': cache_k, 'cache_v': cache_v, 'cache_logf': cache_logf,
            'state_ssm_re': state_ssm_re, 'state_ssm_im': state_ssm_im,
            'g_in': g_in, 'w_in': w_in, 'b_f': b_f, 'a_re': a_re, 'a_im': a_im, 'log_dt': log_dt,
            'b_re': b_re, 'b_im': b_im, 'c_re': c_re, 'c_im': c_im, 'd_skip': d_skip, 'w_glu': w_glu,
            'w_out': w_out, 'w_pe': w_pe, 'g_pe': g_pe, 'w_pg': w_pg, 'g_final': g_final}


def reference(x_prompt, x_sample, p_prompt, p_sample, cache_k, cache_v, cache_logf,
              state_ssm_re, state_ssm_im, g_in, w_in, b_f, a_re, a_im, log_dt, b_re, b_im,
              c_re, c_im, d_skip, w_glu, w_out, w_pe, g_pe, w_pg, g_final):
    y_prompt, k_p, v_p, lf_p, sr_p, si_p = _trunk(
        x_prompt, p_prompt, None, g_in, w_in, b_f, a_re, a_im, log_dt, b_re, b_im, c_re, c_im,
        d_skip, w_glu, w_out, w_pe, g_pe, w_pg, g_final)
    y_sample, k_s, v_s, lf_s, sr_s, si_s = _trunk(
        x_sample, p_sample, (cache_k, cache_v, cache_logf, state_ssm_re, state_ssm_im),
        g_in, w_in, b_f, a_re, a_im, log_dt, b_re, b_im, c_re, c_im,
        d_skip, w_glu, w_out, w_pe, g_pe, w_pg, g_final)
    return (y_prompt, y_sample, k_p, v_p, lf_p, sr_p, si_p, k_s, v_s, lf_s, sr_s, si_s)
```

```python
import functools
import math

import jax
import jax.numpy as jnp
from jax import lax
from jax.experimental import pallas as pl
from jax.experimental.pallas import tpu as pltpu

EPS = 1e-6
NEG_INF = -1e30
F32 = jnp.float32
BF16 = jnp.bfloat16

LANES_V7X = 128
VMEM_BYTES_V7X = 64 * 1024 * 1024
VMEM_LIMIT_CAP = VMEM_BYTES_V7X - 8 * 1024 * 1024

S5_CHUNK = 32
CUMSUM_CHUNK = 256
ATTN_BLOCK = 512
CACHE_BLOCK = 1024


def _params(semantics, vmem_bytes):
    limit = int(min(VMEM_LIMIT_CAP, max(32 * 1024 * 1024, vmem_bytes)))
    return pltpu.CompilerParams(dimension_semantics=semantics, vmem_limit_bytes=limit)


def _resident(shape):
    nd = len(shape)
    return pl.BlockSpec(shape, lambda *_: (0,) * nd, pipeline_mode=pl.Buffered(1))


def _row_tile(n_rows, want):
    t = min(want, n_rows)
    assert n_rows % t == 0, (n_rows, t)
    return t


def _qkv_kernel(x_ref, g_ref, w_ref, wf_ref, bf_ref,
                n_ref, q_ref, k_ref, v_ref, kb_ref, vb_ref, lf_ref, *, n_heads, head_dim):
    d_att = n_heads * head_dim
    x = x_ref[...]
    ms = jnp.mean(x * x, axis=-1, keepdims=True)
    n = (x * lax.rsqrt(ms + EPS) * g_ref[...]).astype(BF16)
    n_ref[...] = n
    q = jnp.dot(n, w_ref[:, 0:d_att], preferred_element_type=F32)
    q_ref[...] = (q * (head_dim ** -0.5)).astype(BF16)
    k = jnp.dot(n, w_ref[:, d_att:2 * d_att], preferred_element_type=F32)
    kb_ref[...] = k.astype(BF16)
    v = jnp.dot(n, w_ref[:, 2 * d_att:3 * d_att], preferred_element_type=F32)
    vb_ref[...] = v.astype(BF16)
    for h in range(n_heads):
        k_ref[:, h, :] = k[:, h * head_dim:(h + 1) * head_dim]
        v_ref[:, h, :] = v[:, h * head_dim:(h + 1) * head_dim]
    fl = jnp.dot(n, wf_ref[...], preferred_element_type=F32) + bf_ref[...]
    lf_ref[...] = jnp.minimum(fl, 0.0) - jnp.log1p(jnp.exp(-jnp.abs(fl)))


def _qkv_call(x2, g_in, w_qkv, w_f, b_f, n_heads, head_dim):
    n_tok, d = x2.shape
    d_att = n_heads * head_dim
    tm = _row_tile(n_tok, 512)
    row = lambda i: (i, 0)
    out_shape = (
        jax.ShapeDtypeStruct((n_tok, d), BF16),
        jax.ShapeDtypeStruct((n_tok, d_att), BF16),
        jax.ShapeDtypeStruct((n_tok, n_heads, head_dim), F32),
        jax.ShapeDtypeStruct((n_tok, n_heads, head_dim), F32),
        jax.ShapeDtypeStruct((n_tok, d_att), BF16),
        jax.ShapeDtypeStruct((n_tok, d_att), BF16),
        jax.ShapeDtypeStruct((n_tok, LANES_V7X), F32),
    )
    vmem = (2 * tm * d * 4 + d * (3 * d_att + LANES_V7X) * 2
            + 2 * (tm * d * 2 + 3 * tm * d_att * 2 + 2 * tm * d_att * 4 + tm * LANES_V7X * 4)
            + 4 * tm * d_att * 4)
    return pl.pallas_call(
        functools.partial(_qkv_kernel, n_heads=n_heads, head_dim=head_dim),
        grid=(n_tok // tm,),
        in_specs=[
            pl.BlockSpec((tm, d), row),
            _resident((1, d)),
            _resident((d, 3 * d_att)),
            _resident((d, LANES_V7X)),
            _resident((1, LANES_V7X)),
        ],
        out_specs=(
            pl.BlockSpec((tm, d), row),
            pl.BlockSpec((tm, d_att), row),
            pl.BlockSpec((tm, n_heads, head_dim), lambda i: (i, 0, 0)),
            pl.BlockSpec((tm, n_heads, head_dim), lambda i: (i, 0, 0)),
            pl.BlockSpec((tm, d_att), row),
            pl.BlockSpec((tm, d_att), row),
            pl.BlockSpec((tm, LANES_V7X), row),
        ),
        out_shape=out_shape,
        compiler_params=_params(("arbitrary",), vmem),
        name="qkv_proj",
    )(x2, g_in, w_qkv, w_f, b_f)


def _gate_kernel(n_ref, w_ref, ga_ref, u_ref, gs_ref, *, d_att, d_ssm):
    n = n_ref[...]
    ga_ref[...] = jnp.dot(n, w_ref[:, 0:d_att], preferred_element_type=F32).astype(BF16)
    u_ref[...] = jnp.dot(n, w_ref[:, d_att:d_att + d_ssm], preferred_element_type=F32).astype(BF16)
    gs_ref[...] = jnp.dot(n, w_ref[:, d_att + d_ssm:], preferred_element_type=F32).astype(BF16)


def _gate_call(n2, w_gug, d_att, d_ssm):
    n_tok, d = n2.shape
    tm = _row_tile(n_tok, 1024)
    row = lambda i: (i, 0)
    wn = d_att + 2 * d_ssm
    vmem = 2 * tm * d * 2 + d * wn * 2 + 2 * tm * wn * 2 + 2 * tm * max(d_att, d_ssm) * 4
    return pl.pallas_call(
        functools.partial(_gate_kernel, d_att=d_att, d_ssm=d_ssm),
        grid=(n_tok // tm,),
        in_specs=[pl.BlockSpec((tm, d), row), _resident((d, wn))],
        out_specs=(pl.BlockSpec((tm, d_att), row), pl.BlockSpec((tm, d_ssm), row),
                   pl.BlockSpec((tm, d_ssm), row)),
        out_shape=(jax.ShapeDtypeStruct((n_tok, d_att), BF16),
                   jax.ShapeDtypeStruct((n_tok, d_ssm), BF16),
                   jax.ShapeDtypeStruct((n_tok, d_ssm), BF16)),
        compiler_params=_params(("arbitrary",), vmem),
        name="gate_proj",
    )(n2, w_gug)


def _negcumsum_kernel(x_ref, o_ref):
    rows, t = x_ref.shape
    ch = CUMSUM_CHUNK
    r = lax.broadcasted_iota(jnp.int32, (ch, ch), 0)
    c = lax.broadcasted_iota(jnp.int32, (ch, ch), 1)
    tri = (r <= c).astype(F32)
    carry = jnp.zeros((rows, 1), F32)
    for i in range(t // ch):
        w = jnp.dot(x_ref[:, i * ch:(i + 1) * ch], tri, precision=lax.Precision.HIGHEST,
                    preferred_element_type=F32) + carry
        o_ref[:, i * ch:(i + 1) * ch] = -w
        carry = w[:, ch - 1:ch]


def _negcumsum_call(x):
    rows, t = x.shape
    assert t % CUMSUM_CHUNK == 0
    return pl.pallas_call(
        _negcumsum_kernel,
        out_shape=jax.ShapeDtypeStruct((rows, t), F32),
        name="neg_cumsum_logf",
    )(x)


def _softmax_step(s, v, m_ref, l_ref, acc_ref):
    m_prev = m_ref[...]
    m_new = jnp.maximum(m_prev, jnp.max(s, axis=1, keepdims=True))
    alpha = jnp.exp(m_prev - m_new)
    p = jnp.exp(s - m_new)
    l_ref[...] = alpha * l_ref[...] + jnp.sum(p, axis=1, keepdims=True)
    acc_ref[...] = alpha * acc_ref[...] + jnp.dot(p.astype(BF16), v, preferred_element_type=F32)
    m_ref[...] = m_new


def _qk(q, k):
    return lax.dot_general(q, k, (((1,), (1,)), ((), ())), preferred_element_type=F32)


def _prompt_attn_kernel(q_ref, k_ref, v_ref, nc_ref, o_ref, m_ref, l_ref, acc_ref, *, blk):
    t = q_ref.shape[1]
    nblk = t // blk
    tri = (lax.broadcasted_iota(jnp.int32, (blk, blk), 1)
           <= lax.broadcasted_iota(jnp.int32, (blk, blk), 0))

    def q_body(qi, _):
        q0 = pl.multiple_of(qi * blk, blk)
        q = q_ref[0, pl.ds(q0, blk), :]
        m_ref[...] = jnp.full(m_ref.shape, NEG_INF, F32)
        l_ref[...] = jnp.zeros(l_ref.shape, F32)
        acc_ref[...] = jnp.zeros(acc_ref.shape, F32)

        def k_body(kj, _):
            k0 = pl.multiple_of(kj * blk, blk)
            s = _qk(q, k_ref[0, pl.ds(k0, blk), :]) + nc_ref[0, 0, :, pl.ds(k0, blk)]
            _softmax_step(s, v_ref[0, pl.ds(k0, blk), :], m_ref, l_ref, acc_ref)
            return 0

        lax.fori_loop(0, qi, k_body, 0)
        s = _qk(q, k_ref[0, pl.ds(q0, blk), :]) + nc_ref[0, 0, :, pl.ds(q0, blk)]
        s = jnp.where(tri, s, NEG_INF)
        _softmax_step(s, v_ref[0, pl.ds(q0, blk), :], m_ref, l_ref, acc_ref)
        o_ref[0, pl.ds(q0, blk), :] = (acc_ref[...] / l_ref[...]).astype(o_ref.dtype)
        return 0

    lax.fori_loop(0, nblk, q_body, 0)


def _prompt_attn_call(q, k, v, negc, n_heads, head_dim):
    b, t, _ = q.shape
    blk = min(ATTN_BLOCK, t)
    assert t % blk == 0
    seq = pl.BlockSpec((1, t, head_dim), lambda bi, hi: (bi, 0, hi))
    vmem = 2 * 4 * t * head_dim * 2 + 2 * t * 4 * 8 + 6 * blk * blk * 4 + 3 * blk * LANES_V7X * 4
    return pl.pallas_call(
        functools.partial(_prompt_attn_kernel, blk=blk),
        grid=(b, n_heads),
        in_specs=[seq, seq, seq, pl.BlockSpec((1, 1, 1, t), lambda bi, hi: (bi, hi, 0, 0))],
        out_specs=seq,
        out_shape=jax.ShapeDtypeStruct(q.shape, BF16),
        scratch_shapes=[pltpu.VMEM((blk, 1), F32), pltpu.VMEM((blk, 1), F32),
                        pltpu.VMEM((blk, head_dim), F32)],
        compiler_params=_params(("arbitrary", "arbitrary"), vmem),
        name="fox_attn_prompt",
    )(q, k, v, negc)


def _sample_attn_kernel(q_ref, kc_ref, vc_ref, ncc_ref, kn_ref, vn_ref, ncn_ref, o_ref,
                        m_ref, l_ref, acc_ref, *, n_heads, head_dim):
    j = pl.program_id(1)
    tq = q_ref.shape[1]

    @pl.when(j == 0)
    def _():
        m_ref[...] = jnp.full(m_ref.shape, NEG_INF, F32)
        l_ref[...] = jnp.zeros(l_ref.shape, F32)
        acc_ref[...] = jnp.zeros(acc_ref.shape, F32)

    for h in range(n_heads):
        hs = slice(h * head_dim, (h + 1) * head_dim)
        q = q_ref[0, :, hs]
        kh = kc_ref[0, 0, :, h, :].astype(BF16)
        vh = vc_ref[0, 0, :, h, :].astype(BF16)
        s = _qk(q, kh) + ncc_ref[0, h]
        _softmax_step(s, vh, m_ref.at[h], l_ref.at[h], acc_ref.at[h])

    @pl.when(j == pl.num_programs(1) - 1)
    def _():
        tri = (lax.broadcasted_iota(jnp.int32, (tq, tq), 1)
               <= lax.broadcasted_iota(jnp.int32, (tq, tq), 0))
        for h in range(n_heads):
            hs = slice(h * head_dim, (h + 1) * head_dim)
            s = _qk(q_ref[0, :, hs], kn_ref[0, :, hs]) + ncn_ref[0, h]
            s = jnp.where(tri, s, NEG_INF)
            _softmax_step(s, vn_ref[0, :, hs], m_ref.at[h], l_ref.at[h], acc_ref.at[h])
            o_ref[0, :, hs] = (acc_ref[h] / l_ref[h]).astype(o_ref.dtype)


def _sample_attn_call(q, cache_k, cache_v, negc_cache, k_new, v_new, negc_new, n_heads, head_dim):
    b, t, da = q.shape
    past = cache_k.shape[2]
    tk = min(CACHE_BLOCK, past)
    assert past % tk == 0
    tok = pl.BlockSpec((1, t, da), lambda bi, j: (bi, 0, 0))
    cache = pl.BlockSpec((1, 1, tk, n_heads, head_dim), lambda bi, j: (0, bi, j, 0, 0))
    vmem = 2 * 2 * tk * da * 4 + 2 * tk * da * 2 + 8 * t * da * 2 + 4 * t * tk * 4 * 4
    return pl.pallas_call(
        functools.partial(_sample_attn_kernel, n_heads=n_heads, head_dim=head_dim),
        grid=(b, past // tk),
        in_specs=[tok, cache, cache,
                  pl.BlockSpec((1, n_heads, 1, tk), lambda bi, j: (bi, 0, 0, j)),
                  tok, tok,
                  pl.BlockSpec((1, n_heads, 1, t), lambda bi, j: (bi, 0, 0, 0))],
        out_specs=tok,
        out_shape=jax.ShapeDtypeStruct(q.shape, BF16),
        scratch_shapes=[pltpu.VMEM((n_heads, t, 1), F32), pltpu.VMEM((n_heads, t, 1), F32),
                        pltpu.VMEM((n_heads, t, head_dim), F32)],
        compiler_params=_params(("arbitrary", "arbitrary"), vmem),
        name="fox_attn_sample",
    )(q, cache_k, cache_v, negc_cache, k_new, v_new, negc_new)


def _cpow(e, ar2, ai2):
    mag = jnp.exp(e * ar2)
    ang = e * ai2
    return mag * jnp.cos(ang), mag * jnp.sin(ang)


def _s5_prep_kernel(ldt_ref, a_ref, bt_ref, c_ref, m1_ref, min_ref, moutt_ref, asc_ref, *, chunk):
    L = chunk
    p2 = a_ref.shape[2] // 2
    p = p2 // 2
    gsz = c_ref.shape[1]
    lc = L * gsz
    dt = jnp.exp(ldt_ref[0])
    a_re2 = a_ref[0, :, 0:p2]
    a_im2 = a_ref[0, :, p2:2 * p2]
    ar2 = a_re2 * dt
    ai2 = a_im2 * dt
    lane = lax.broadcasted_iota(jnp.int32, (1, p2), 1)
    sgn = jnp.where(lane < p, 1.0, -1.0).astype(F32)

    ab_re, ab_im = _cpow(jnp.ones((1, 1), F32), ar2, ai2)
    x, y = ab_re - 1.0, ab_im
    den = a_re2 * a_re2 + a_im2 * a_im2
    cf_re = (x * a_re2 + y * a_im2) / den
    cf_im = (y * a_re2 - x * a_im2) / den
    bt = bt_ref[0]
    bbar = cf_re * bt[:, 0:p2] + cf_im * bt[:, p2:2 * p2]
    bbar_sw = pltpu.roll(bbar, p, axis=1) * (-sgn)

    cc = c_ref[0]
    tile = lambda m: jnp.broadcast_to(m[None], (L,) + m.shape).reshape(L * m.shape[0], m.shape[1])
    c_a, c_b = tile(cc[:, 0:p2]), tile(cc[:, p2:2 * p2])
    assert gsz & (gsz - 1) == 0
    tok = lax.shift_right_logical(lax.broadcasted_iota(jnp.int32, (lc, 1), 0),
                                  int(math.log2(gsz))).astype(F32)

    pw_re, pw_im = _cpow(tok, ar2, ai2)
    et = pw_re * c_a + pw_im * c_b
    r0 = lax.dot_general(bbar * sgn, et, (((1,), (1,)), ((), ())),
                         precision=lax.Precision.HIGHEST, preferred_element_type=F32)
    col = lax.broadcasted_iota(jnp.int32, (gsz, lc), 1)
    for j in range(L):
        blk = r0 if j == 0 else jnp.where(col >= j * gsz, pltpu.roll(r0, j * gsz, axis=1), 0.0)
        m1_ref[0, j * gsz:(j + 1) * gsz, :] = blk.astype(m1_ref.dtype)

    pw_re, pw_im = _cpow((L - 1.0) - tok, ar2, ai2)
    min_ref[0] = (pw_re * tile(bbar) + pw_im * tile(bbar_sw)).astype(min_ref.dtype)

    pw_re, pw_im = _cpow(tok + 1.0, ar2, ai2)
    moutt_ref[0] = ((pw_re * c_a + pw_im * c_b) * sgn).astype(moutt_ref.dtype)

    kk = lax.broadcasted_iota(jnp.int32, (8, 1), 0)
    e = (L * jnp.left_shift(1, kk)).astype(F32)
    pw_re, pw_im = _cpow(e, ar2, ai2)
    asc_ref[0, 0:8, :] = pw_re
    asc_ref[0, 8:16, :] = -pw_im * sgn


def _s5_prep_call(log_dt, a_re, a_im, b_re, b_im, c_re, c_im, chunk):
    g, p = a_re.shape
    gsz = c_re.shape[1]
    p2 = 2 * p
    lc = chunk * gsz
    ldt = jnp.broadcast_to(log_dt.reshape(g, 1, 1), (g, 1, p2))
    a_pack = jnp.concatenate([a_re, a_re, a_im, a_im], axis=-1).reshape(g, 1, 2 * p2)
    bt_re, bt_im = jnp.swapaxes(b_re, 1, 2), jnp.swapaxes(b_im, 1, 2)
    bt_pack = jnp.concatenate([bt_re, bt_im, -bt_im, bt_re], axis=-1)
    c_pack = jnp.concatenate([c_re, c_im, -c_im, c_re], axis=-1)
    grp = lambda shape: pl.BlockSpec((1,) + shape, lambda i: (i, 0, 0))
    return pl.pallas_call(
        functools.partial(_s5_prep_kernel, chunk=chunk),
        grid=(g,),
        in_specs=[grp((1, p2)), grp((1, 2 * p2)), grp((gsz, 2 * p2)), grp((gsz, 2 * p2))],
        out_specs=(grp((lc, lc)), grp((lc, p2)), grp((lc, p2)), grp((16, p2))),
        out_shape=(jax.ShapeDtypeStruct((g, lc, lc), BF16),
                   jax.ShapeDtypeStruct((g, lc, p2), BF16),
                   jax.ShapeDtypeStruct((g, lc, p2), BF16),
                   jax.ShapeDtypeStruct((g, 16, p2), F32)),
        compiler_params=_params(("arbitrary",), 0),
        name="s5_prep",
    )(ldt, a_pack, bt_pack, c_pack)


def _gelu_tanh(y):
    return 0.5 * y * (1.0 + jnp.tanh(math.sqrt(2.0 / math.pi) * (y + 0.044715 * (y * y * y))))


def _s5_kernel(*refs, n_chunks, has_h0):
    if has_h0:
        x_ref, m1_ref, min_ref, moutt_ref, asc_ref, d_ref, h0_ref, y_ref, hl_ref = refs
    else:
        x_ref, m1_ref, min_ref, moutt_ref, asc_ref, d_ref, y_ref, hl_ref = refs
    x = x_ref[0]
    rows = x.shape[0]
    p2 = min_ref.shape[2]
    nb = rows // n_chunks
    swap = lambda z: pltpu.roll(z, p2 // 2, axis=1)
    e = jnp.dot(x, min_ref[0], preferred_element_type=F32)
    if has_h0:
        h_in = h0_ref[0]
        e = e + asc_ref[0, 0:1, :] * h_in + asc_ref[0, 8:9, :] * swap(h_in)
    else:
        cidx = lax.broadcasted_iota(jnp.int32, (rows, 1), 0) & (n_chunks - 1)
        for k in range(int(math.log2(n_chunks))):
            sh = 1 << k
            prev = jnp.where(cidx >= sh, pltpu.roll(e, sh, axis=0), 0.0)
            e = e + asc_ref[0, k:k + 1, :] * prev + asc_ref[0, 8 + k:9 + k, :] * swap(prev)
        h_in = jnp.where(cidx >= 1, pltpu.roll(e, 1, axis=0), 0.0)
    y = jnp.dot(x, m1_ref[0], preferred_element_type=F32)
    y = y + lax.dot_general(h_in.astype(BF16), moutt_ref[0], (((1,), (1,)), ((), ())),
                            preferred_element_type=F32)
    y = y + d_ref[0] * x.astype(F32)
    y_ref[0] = _gelu_tanh(y)
    for b in range(nb):
        hl_ref[0, b:b + 1, :] = e[(b + 1) * n_chunks - 1:(b + 1) * n_chunks, :]


def _s5_call(xg, mats, d_vec, h0, n_chunks):
    m1, mn, moutt, asc = mats
    g, rows, lc = xg.shape
    p2 = mn.shape[2]
    nb = rows // n_chunks
    assert n_chunks & (n_chunks - 1) == 0 and n_chunks <= 256
    assert h0 is None or n_chunks == 1
    grp = lambda shape: pl.BlockSpec((1,) + shape, lambda i: (i, 0, 0))
    in_specs = [grp((rows, lc)), grp((lc, lc)), grp((lc, p2)), grp((lc, p2)), grp((16, p2)),
                grp((1, lc))]
    args = [xg, m1, mn, moutt, asc, d_vec]
    if h0 is not None:
        in_specs.append(grp((nb, p2)))
        args.append(h0)
    vmem = 2 * rows * lc * (2 + 4) + 8 * rows * lc * 4 + 4 * lc * lc * 2
    return pl.pallas_call(
        functools.partial(_s5_kernel, n_chunks=n_chunks, has_h0=h0 is not None),
        grid=(g,),
        in_specs=in_specs,
        out_specs=(grp((rows, lc)), grp((nb, p2))),
        out_shape=(jax.ShapeDtypeStruct((g, rows, lc), F32),
                   jax.ShapeDtypeStruct((g, nb, p2), F32)),
        compiler_params=_params(("arbitrary",), vmem),
        name="s5_chunked",
    )(*args)


def _sigmoid(z):
    return 1.0 / (1.0 + jnp.exp(-z))


def _out_kernel(x_ref, att_ref, ga_ref, s_ref, gs_ref, pe_ref,
                wglu_ref, wout_ref, wpe_ref, wpg_ref, gpe_ref, gfin_ref, y_ref, *, d_att):
    s = s_ref[...]
    z = jnp.dot(s.astype(BF16), wglu_ref[...], preferred_element_type=F32)
    ssm = s * _sigmoid(z)
    gs = gs_ref[...].astype(F32)
    m_s = (ssm * (gs * _sigmoid(gs))).astype(BF16)
    ga = ga_ref[...].astype(F32)
    m_a = (att_ref[...].astype(F32) * (ga * _sigmoid(ga))).astype(BF16)
    h = x_ref[...] + jnp.dot(m_a, wout_ref[0:d_att, :], preferred_element_type=F32)
    h = h + jnp.dot(m_s, wout_ref[d_att:, :], preferred_element_type=F32)
    e = jnp.dot(pe_ref[...].astype(BF16), wpe_ref[...], preferred_element_type=F32)
    e = e * lax.rsqrt(jnp.mean(e * e, axis=-1, keepdims=True) + EPS) * gpe_ref[...]
    gate = _sigmoid(jnp.dot(h.astype(BF16), wpg_ref[...], preferred_element_type=F32))
    h = h + e * gate
    y_ref[...] = h * lax.rsqrt(jnp.mean(h * h, axis=-1, keepdims=True) + EPS) * gfin_ref[...]


def _out_call(x2, att, ga, s, gs, pe, w_glu, w_out, w_pe, w_pg, g_pe, g_fin):
    n_tok, d = x2.shape
    d_att, d_ssm, d_ple = att.shape[1], s.shape[1], pe.shape[1]
    tm = _row_tile(n_tok, 256)
    row = lambda i: (i, 0)
    vmem = ((d_ssm * d_ssm + (d_att + d_ssm) * d + d_ple * d + d * d) * 2
            + 2 * tm * (2 * d * 4 + 2 * d_att * 2 + d_ssm * 4 + d_ssm * 2 + d_ple * 4)
            + 6 * tm * d * 4)
    return pl.pallas_call(
        functools.partial(_out_kernel, d_att=d_att),
        grid=(n_tok // tm,),
        in_specs=[pl.BlockSpec((tm, d), row), pl.BlockSpec((tm, d_att), row),
                  pl.BlockSpec((tm, d_att), row), pl.BlockSpec((tm, d_ssm), row),
                  pl.BlockSpec((tm, d_ssm), row), pl.BlockSpec((tm, d_ple), row),
                  _resident((d_ssm, d_ssm)), _resident((d_att + d_ssm, d)),
                  _resident((d_ple, d)), _resident((d, d)), _resident((1, d)), _resident((1, d))],
        out_specs=pl.BlockSpec((tm, d), row),
        out_shape=jax.ShapeDtypeStruct((n_tok, d), F32),
        compiler_params=_params(("arbitrary",), vmem),
        name="out_proj_ple_norm",
    )(x2, att, ga, s, gs, pe, w_glu, w_out, w_pe, w_pg, g_pe, g_fin)


def _pad_lanes(x, mult):
    pad = (-x.shape[-1]) % mult
    return x if pad == 0 else jnp.pad(x, [(0, 0)] * (x.ndim - 1) + [(0, pad)])


def _trunk(x, pe, past, wts, s5_mats, dims):
    n_heads, head_dim, n_groups, gsz, state = dims
    b, t, d = x.shape
    d_att, d_ssm = n_heads * head_dim, n_groups * gsz
    n_tok = b * t
    x2 = x.reshape(n_tok, d)
    n2, q, k32, v32, kb, vb, lf = _qkv_call(x2, wts["g_in"], wts["w_qkv"], wts["w_f"], wts["b_f"],
                                            n_heads, head_dim)
    ga, u, gs = _gate_call(n2, wts["w_gug"], d_att, d_ssm)
    logf = lf[:, :n_heads].reshape(b, t, n_heads)
    lf_rows = jnp.swapaxes(logf, 1, 2)
    seq = lambda a: a.reshape(b, t, d_att)

    if past is None:
        negc = _negcumsum_call(lf_rows.reshape(b * n_heads, t)).reshape(b, n_heads, 1, t)
        att = _prompt_attn_call(seq(q), seq(kb), seq(vb), negc, n_heads, head_dim)
        h0 = None
    else:
        cache_k, cache_v, cache_logf, h0_re, h0_im = past
        n_past = cache_k.shape[2]
        lf_all = jnp.concatenate([jnp.swapaxes(cache_logf[0], 1, 2), lf_rows], axis=2)
        lf_all = _pad_lanes(lf_all, CUMSUM_CHUNK)
        negc = _negcumsum_call(lf_all.reshape(b * n_heads, -1)).reshape(b, n_heads, 1, -1)
        att = _sample_attn_call(seq(q), cache_k, cache_v, negc[..., :n_past], seq(kb), seq(vb),
                                negc[..., n_past:n_past + t], n_heads, head_dim)
        h0 = jnp.swapaxes(jnp.concatenate([h0_re[0], h0_im[0]], axis=-1), 0, 1)

    L = S5_CHUNK
    n_chunks = t // L
    xg = u.reshape(b * n_chunks, L, n_groups, gsz).transpose(2, 0, 1, 3).reshape(
        n_groups, b * n_chunks, L * gsz)
    yg, hl = _s5_call(xg, s5_mats, wts["d_vec"], h0, n_chunks)
    s = yg.reshape(n_groups, b * n_chunks, L, gsz).transpose(1, 2, 0, 3).reshape(n_tok, d_ssm)
    hl = jnp.swapaxes(hl, 0, 1)

    y = _out_call(x2, att.reshape(n_tok, d_att), ga, s, gs, pe.reshape(n_tok, -1),
                  wts["w_glu"], wts["w_out"], wts["w_pe"], wts["w_pg"], wts["g_pe"], wts["g_final"])
    return (y.reshape(b, t, d),
            k32.reshape(1, b, t, n_heads, head_dim), v32.reshape(1, b, t, n_heads, head_dim),
            logf[None], hl[None, :, :, :state], hl[None, :, :, state:])


def kernel(x_prompt, x_sample, p_prompt, p_sample, cache_k, cache_v, cache_logf, state_ssm_re, state_ssm_im, g_in, w_in, b_f, a_re, a_im, log_dt, b_re, b_im, c_re, c_im, d_skip, w_glu, w_out, w_pe, g_pe, w_pg, g_final):
    assert g_in.shape[0] == 1, "single-layer trunk"
    n_heads, head_dim = cache_k.shape[3], cache_k.shape[4]
    n_groups, state = a_re.shape[1], a_re.shape[2]
    gsz = b_re.shape[3]
    d = x_prompt.shape[-1]
    d_att, d_ssm = n_heads * head_dim, n_groups * gsz
    assert 2 * state == LANES_V7X and head_dim == LANES_V7X
    dims = (n_heads, head_dim, n_groups, gsz, state)

    off_f = 3 * d_att
    off_ga = off_f + n_heads
    w = w_in[0]
    wts = {
        "g_in": g_in[0].reshape(1, d),
        "w_qkv": w[:, :off_f].astype(BF16),
        "w_f": _pad_lanes(w[:, off_f:off_ga], LANES_V7X).astype(BF16),
        "b_f": _pad_lanes(b_f[0].reshape(1, n_heads), LANES_V7X),
        "w_gug": w[:, off_ga:].astype(BF16),
        "d_vec": jnp.tile(d_skip[0].reshape(n_groups, 1, gsz), (1, 1, S5_CHUNK)),
        "w_glu": w_glu[0].astype(BF16),
        "w_out": w_out[0].astype(BF16),
        "w_pe": w_pe[0].astype(BF16),
        "w_pg": w_pg[0].astype(BF16),
        "g_pe": g_pe[0].reshape(1, d),
        "g_final": g_final.reshape(1, d),
    }
    s5_mats = _s5_prep_call(log_dt[0], a_re[0], a_im[0], b_re[0], b_im[0], c_re[0], c_im[0], S5_CHUNK)

    out_p = _trunk(x_prompt, p_prompt[0], None, wts, s5_mats, dims)
    out_s = _trunk(x_sample, p_sample[0],
                   (cache_k, cache_v, cache_logf, state_ssm_re, state_ssm_im), wts, s5_mats, dims)
    return (out_p[0], out_s[0]) + out_p[1:] + out_s[1:]
```

```python
import functools
import math

import jax
import jax.numpy as jnp
from jax import lax
from jax.experimental import pallas as pl
from jax.experimental.pallas import tpu as pltpu

EPS = 1e-6
NEG_INF = -1e30
LOG2E = math.log2(math.e)
F32 = jnp.float32
BF16 = jnp.bfloat16

LANES_V7X = 128
VMEM_BYTES_V7X = 64 * 1024 * 1024
VMEM_LIMIT_CAP = VMEM_BYTES_V7X - 8 * 1024 * 1024

S5_CHUNK = 32
CUMSUM_CHUNK = 256
ATTN_BLOCK = 512
ATTN_Q_BLOCK = 2048
ATTN_Q_TILE = 256
ATTN_KEY_UNROLL = 2
ATTN_LOOKAHEAD = 3
CACHE_BLOCK = 1024


def _params(semantics, vmem_bytes):
    limit = int(min(VMEM_LIMIT_CAP, max(32 * 1024 * 1024, vmem_bytes)))
    return pltpu.CompilerParams(dimension_semantics=semantics, vmem_limit_bytes=limit)


def _resident(shape):
    nd = len(shape)
    return pl.BlockSpec(shape, lambda *_: (0,) * nd, pipeline_mode=pl.Buffered(1))


def _row_tile(n_rows, want):
    t = min(want, n_rows)
    assert n_rows % t == 0, (n_rows, t)
    return t


def _qkv_kernel(x_ref, g_ref, w_ref, wf_ref, bf_ref,
                n_ref, q_ref, k_ref, v_ref, kb_ref, vb_ref, lf_ref, *, n_heads, head_dim, transposed):
    d_att = n_heads * head_dim
    x = x_ref[...]
    ms = jnp.mean(x * x, axis=-1, keepdims=True)
    n = (x * lax.rsqrt(ms + EPS) * g_ref[...]).astype(BF16)
    n_ref[...] = n
    q = jnp.dot(n, w_ref[:, 0:d_att], preferred_element_type=F32)
    k = jnp.dot(n, w_ref[:, d_att:2 * d_att], preferred_element_type=F32)
    kb_ref[...] = k.astype(BF16)
    v = jnp.dot(n, w_ref[:, 2 * d_att:3 * d_att], preferred_element_type=F32)
    q = q * (head_dim ** -0.5 * LOG2E)
    if transposed:
        q_ref[...] = q.T.astype(BF16)
        vb_ref[...] = v.T.astype(BF16)
    else:
        q_ref[...] = q.astype(BF16)
        vb_ref[...] = v.astype(BF16)
    tm = x.shape[0]
    for h in range(n_heads):
        k_ref[pl.ds(h, tm, stride=n_heads), :] = k[:, h * head_dim:(h + 1) * head_dim]
        v_ref[pl.ds(h, tm, stride=n_heads), :] = v[:, h * head_dim:(h + 1) * head_dim]
    fl = jnp.dot(n, wf_ref[...], preferred_element_type=F32) + bf_ref[...]
    lf_ref[...] = jnp.minimum(fl, 0.0) - jnp.log1p(jnp.exp(-jnp.abs(fl)))


def _qkv_call(x2, g_in, w_qkv, w_f, b_f, n_heads, head_dim, transposed):
    n_tok, d = x2.shape
    d_att = n_heads * head_dim
    tm = _row_tile(n_tok, 512)
    row = lambda i: (i, 0)
    if transposed:
        qv_shape, qv_spec = (d_att, n_tok), pl.BlockSpec((d_att, tm), lambda i: (0, i))
    else:
        qv_shape, qv_spec = (n_tok, d_att), pl.BlockSpec((tm, d_att), row)
    out_shape = (
        jax.ShapeDtypeStruct((n_tok, d), BF16),
        jax.ShapeDtypeStruct(qv_shape, BF16),
        jax.ShapeDtypeStruct((n_tok * n_heads, head_dim), F32),
        jax.ShapeDtypeStruct((n_tok * n_heads, head_dim), F32),
        jax.ShapeDtypeStruct((n_tok, d_att), BF16),
        jax.ShapeDtypeStruct(qv_shape, BF16),
        jax.ShapeDtypeStruct((n_tok, LANES_V7X), F32),
    )
    vmem = (2 * tm * d * 4 + d * (3 * d_att + LANES_V7X) * 2
            + 2 * (tm * d * 2 + 3 * tm * d_att * 2 + 2 * tm * d_att * 4 + tm * LANES_V7X * 4)
            + 6 * tm * d_att * 4)
    return pl.pallas_call(
        functools.partial(_qkv_kernel, n_heads=n_heads, head_dim=head_dim, transposed=transposed),
        grid=(n_tok // tm,),
        in_specs=[
            pl.BlockSpec((tm, d), row),
            _resident((1, d)),
            _resident((d, 3 * d_att)),
            _resident((d, LANES_V7X)),
            _resident((1, LANES_V7X)),
        ],
        out_specs=(
            pl.BlockSpec((tm, d), row),
            qv_spec,
            pl.BlockSpec((tm * n_heads, head_dim), row),
            pl.BlockSpec((tm * n_heads, head_dim), row),
            pl.BlockSpec((tm, d_att), row),
            qv_spec,
            pl.BlockSpec((tm, LANES_V7X), row),
        ),
        out_shape=out_shape,
        compiler_params=_params(("arbitrary",), vmem),
        name="qkv_proj",
    )(x2, g_in, w_qkv, w_f, b_f)


def _gate_kernel(n_ref, w_ref, ga_ref, u_ref, gs_ref, *, d_att, d_ssm):
    n = n_ref[...]
    ga_ref[...] = jnp.dot(n, w_ref[:, 0:d_att], preferred_element_type=F32).astype(BF16)
    u_ref[...] = jnp.dot(n, w_ref[:, d_att:d_att + d_ssm], preferred_element_type=F32).astype(BF16)
    gs_ref[...] = jnp.dot(n, w_ref[:, d_att + d_ssm:], preferred_element_type=F32).astype(BF16)


def _gate_call(n2, w_gug, d_att, d_ssm):
    n_tok, d = n2.shape
    tm = _row_tile(n_tok, 1024)
    row = lambda i: (i, 0)
    wn = d_att + 2 * d_ssm
    vmem = 2 * tm * d * 2 + d * wn * 2 + 2 * tm * wn * 2 + 2 * tm * max(d_att, d_ssm) * 4
    return pl.pallas_call(
        functools.partial(_gate_kernel, d_att=d_att, d_ssm=d_ssm),
        grid=(n_tok // tm,),
        in_specs=[pl.BlockSpec((tm, d), row), _resident((d, wn))],
        out_specs=(pl.BlockSpec((tm, d_att), row), pl.BlockSpec((tm, d_ssm), row),
                   pl.BlockSpec((tm, d_ssm), row)),
        out_shape=(jax.ShapeDtypeStruct((n_tok, d_att), BF16),
                   jax.ShapeDtypeStruct((n_tok, d_ssm), BF16),
                   jax.ShapeDtypeStruct((n_tok, d_ssm), BF16)),
        compiler_params=_params(("arbitrary",), vmem),
        name="gate_proj",
    )(n2, w_gug)


def _negcumsum_kernel(x_ref, o_ref):
    rows, t = x_ref.shape
    ch = CUMSUM_CHUNK
    r = lax.broadcasted_iota(jnp.int32, (ch, ch), 0)
    c = lax.broadcasted_iota(jnp.int32, (ch, ch), 1)
    tri = (r <= c).astype(F32)
    carry = jnp.zeros((rows, 1), F32)
    for i in range(t // ch):
        w = jnp.dot(x_ref[:, i * ch:(i + 1) * ch], tri, precision=lax.Precision.HIGHEST,
                    preferred_element_type=F32) + carry
        o_ref[:, i * ch:(i + 1) * ch] = -w
        carry = w[:, ch - 1:ch]


def _negcumsum_call(x):
    rows, t = x.shape
    assert t % CUMSUM_CHUNK == 0
    return pl.pallas_call(
        _negcumsum_kernel,
        out_shape=jax.ShapeDtypeStruct((rows, t), F32),
        name="neg_cumsum_logf",
    )(x)


def _qk(q, k):
    return lax.dot_general(q, k, (((1,), (1,)), ((), ())), preferred_element_type=F32)


BIAS_PIECES = 3


def _bias_cols_kernel(x_ref, o_ref, *, n_heads):
    t = x_ref.shape[1]
    ch = CUMSUM_CHUNK
    lanes = x_ref.shape[2]
    tri = (lax.broadcasted_iota(jnp.int32, (ch, ch), 1)
           <= lax.broadcasted_iota(jnp.int32, (ch, ch), 0)).astype(F32)
    d = (lax.broadcasted_iota(jnp.int32, (lanes, lanes), 1)
         - BIAS_PIECES * lax.broadcasted_iota(jnp.int32, (lanes, lanes), 0))
    head = lax.broadcasted_iota(jnp.int32, (lanes, lanes), 0)
    sel = ((d >= 0) & (d < BIAS_PIECES) & (head < n_heads)).astype(F32)
    lane = lax.broadcasted_iota(jnp.int32, (1, lanes), 1)
    piece = lane
    for h in range(1, n_heads):
        piece = piece - jnp.where(lane >= BIAS_PIECES * h, BIAS_PIECES, 0)

    def body(i, carry):
        r0 = pl.multiple_of(i * ch, ch)
        xr = jnp.dot(x_ref[0, pl.ds(r0, ch), :], sel, precision=lax.Precision.HIGHEST,
                     preferred_element_type=F32)
        c = jnp.dot(tri, xr, precision=lax.Precision.HIGHEST, preferred_element_type=F32) + carry
        z = c * (-LOG2E)
        hi = z.astype(BF16)
        r1 = z - hi.astype(F32)
        mid = r1.astype(BF16)
        lo = (r1 - mid.astype(F32)).astype(BF16)
        out = jnp.where(piece == 0, hi, jnp.where(piece == 1, mid, lo))
        o_ref[0, pl.ds(r0, ch), :] = jnp.where(lane < BIAS_PIECES * n_heads, out, jnp.zeros_like(out))
        return c[ch - 1:ch, :]

    lax.fori_loop(0, t // ch, body, jnp.zeros((1, lanes), F32))


def _bias_cols_call(lf3, n_heads):
    b, t, lanes = lf3.shape
    assert t % CUMSUM_CHUNK == 0 and BIAS_PIECES * n_heads <= lanes
    spec = pl.BlockSpec((1, t, lanes), lambda bi: (bi, 0, 0))
    return pl.pallas_call(
        functools.partial(_bias_cols_kernel, n_heads=n_heads),
        grid=(b,),
        in_specs=[spec],
        out_specs=spec,
        out_shape=jax.ShapeDtypeStruct((b, t, lanes), BF16),
        compiler_params=_params(("arbitrary",), 0),
        name="forget_bias_cols",
    )(lf3)


def _prompt_attn_kernel(qt_ref, k_ref, cx_ref, vt_ref, o_ref, m_ref, acc_ref, st_ref, *, blk, head_dim):
    t = k_ref.shape[1]
    tq = m_ref.shape[1]
    ratio = tq // blk
    unroll = math.gcd(ratio, ATTN_KEY_UNROLL)
    h = pl.program_id(1)
    ext_row = lax.broadcasted_iota(jnp.int32, (head_dim, ATTN_Q_TILE), 0)
    q_ext = ((ext_row >= BIAS_PIECES * h) & (ext_row < BIAS_PIECES * (h + 1))).astype(BF16)
    ones_rows = jnp.ones((16, blk), BF16)

    def q_body(qi, _):
        q0 = pl.multiple_of(qi * tq, tq)
        m_ref[...] = jnp.full(m_ref.shape, NEG_INF, F32)
        acc_ref[...] = jnp.zeros(acc_ref.shape, F32)

        def scores(item):
            k0, c0, _ = item
            k_aug = jnp.concatenate([k_ref[0, pl.ds(k0, blk), :], cx_ref[0, pl.ds(k0, blk), :]], axis=1)
            q_aug = jnp.concatenate([qt_ref[:, pl.ds(q0 + c0, ATTN_Q_TILE)], q_ext], axis=0)
            return jnp.dot(k_aug, q_aug, preferred_element_type=F32)

        def run(items, next_items):
            assert len(items) >= ATTN_LOOKAHEAD and len(next_items) in (0, ATTN_LOOKAHEAD)
            seq = items + next_items
            pending = [st_ref[i] for i in range(ATTN_LOOKAHEAD)]
            for i, (k0, c0, shift) in enumerate(items):
                st = pending.pop(0)
                if i + ATTN_LOOKAHEAD < len(seq):
                    pending.append(scores(seq[i + ATTN_LOOKAHEAD]))
                if shift is not None:
                    key = lax.broadcasted_iota(jnp.int32, st.shape, 0)
                    qry = lax.broadcasted_iota(jnp.int32, st.shape, 1) + shift
                    st = jnp.where(key <= qry, st, NEG_INF)
                cols = slice(c0, c0 + ATTN_Q_TILE)
                m_prev = m_ref[:, cols]
                m_new = jnp.maximum(m_prev, jnp.max(st, axis=0, keepdims=True))
                p = jnp.exp2(st - m_new).astype(BF16)
                vt_aug = jnp.concatenate([vt_ref[:, pl.ds(k0, blk)], ones_rows], axis=0)
                acc_ref[:, cols] = (jnp.exp2(m_prev - m_new) * acc_ref[:, cols]
                                    + jnp.dot(vt_aug, p, preferred_element_type=F32))
                m_ref[:, cols] = m_new
            for i, st in enumerate(pending):
                st_ref[i] = st

        def head_items(k0):
            return [(k0, i * ATTN_Q_TILE, None) for i in range(ATTN_LOOKAHEAD)]

        def k_body(kj, _):
            k0 = pl.multiple_of(kj * (unroll * blk), unroll * blk)
            run([(k0 + u * blk, c0, None) for u in range(unroll) for c0 in range(0, tq, ATTN_Q_TILE)],
                head_items(k0 + unroll * blk))
            return 0

        for i, item in enumerate(head_items(0)):
            st_ref[i] = scores(item)
        lax.fori_loop(0, qi * (ratio // unroll), k_body, 0)
        run([(q0 + r * blk, c0, c0 - r * blk if c0 < (r + 1) * blk else None)
             for r in range(ratio) for c0 in range(r * blk, tq, ATTN_Q_TILE)], [])
        acc = acc_ref[...]
        ot = acc[0:head_dim, :] / acc[head_dim:head_dim + 1, :]
        o_ref[0, pl.ds(q0, tq), :] = ot.T.astype(o_ref.dtype)
        return 0

    lax.fori_loop(0, t // tq, q_body, 0)


def _prompt_attn_call(qt, k, cx, vt, b, n_heads, head_dim):
    t = k.shape[1]
    blk = min(ATTN_BLOCK, t)
    tq = min(ATTN_Q_BLOCK, t)
    assert t % tq == 0 and tq % blk == 0 and blk % ATTN_Q_TILE == 0
    seq = pl.BlockSpec((1, t, head_dim), lambda bi, hi: (bi, 0, hi))
    seq_t = pl.BlockSpec((head_dim, t), lambda bi, hi: (hi, bi))
    vmem = 2 * 5 * t * head_dim * 2 + 16 * blk * ATTN_Q_TILE * 4 + 2 * (head_dim + 16) * tq * 4
    return pl.pallas_call(
        functools.partial(_prompt_attn_kernel, blk=blk, head_dim=head_dim),
        grid=(b, n_heads),
        in_specs=[seq_t, seq, pl.BlockSpec((1, t, LANES_V7X), lambda bi, hi: (bi, 0, 0)), seq_t],
        out_specs=seq,
        out_shape=jax.ShapeDtypeStruct(k.shape, BF16),
        scratch_shapes=[pltpu.VMEM((1, tq), F32), pltpu.VMEM((head_dim + 16, tq), F32),
                        pltpu.VMEM((ATTN_LOOKAHEAD, blk, ATTN_Q_TILE), F32)],
        compiler_params=_params(("arbitrary", "arbitrary"), vmem),
        name="fox_attn_prompt",
    )(qt, k, cx, vt)


def _sample_attn_kernel(q_ref, kc_ref, vc_ref, ncc_ref, kn_ref, vn_ref, ncn_ref, o_ref,
                        qbd_ref, m_ref, l_ref, acc_ref, *, n_heads, head_dim):
    j = pl.program_id(1)
    tq = q_ref.shape[1]
    tk = kc_ref.shape[1] // n_heads
    rows = n_heads * tq

    @pl.when(j == 0)
    def _():
        q_rep = jnp.concatenate([q_ref[0]] * n_heads, axis=0)
        r_head = lax.broadcasted_iota(jnp.int32, q_rep.shape, 0) // tq
        c_head = lax.broadcasted_iota(jnp.int32, q_rep.shape, 1) // head_dim
        qbd_ref[...] = jnp.where(r_head == c_head, q_rep, jnp.zeros_like(q_rep))
        m_ref[...] = jnp.full(m_ref.shape, NEG_INF, F32)
        l_ref[...] = jnp.zeros(l_ref.shape, F32)
        acc_ref[...] = jnp.zeros(acc_ref.shape, F32)

    def step(s, values):
        m_prev = m_ref[...]
        m_new = jnp.maximum(m_prev, jnp.max(s, axis=1, keepdims=True))
        alpha = jnp.exp2(m_prev - m_new)
        p = jnp.exp2(s - m_new)
        l_ref[...] = alpha * l_ref[...] + jnp.sum(p, axis=1, keepdims=True)
        m_ref[...] = m_new
        p = p.astype(BF16)
        for h in range(n_heads):
            rs = slice(h * tq, (h + 1) * tq)
            acc_ref[h] = alpha[rs] * acc_ref[h] + jnp.dot(p[rs], values[h], preferred_element_type=F32)

    def head_bias(nc):
        return jnp.broadcast_to((nc * LOG2E)[:, None, :], (n_heads, tq, nc.shape[1])).reshape(rows, nc.shape[1])

    k2d = jnp.concatenate([kc_ref[0, pl.ds(h, tk, stride=n_heads), :].astype(BF16)
                           for h in range(n_heads)], axis=1)
    vals = [vc_ref[0, pl.ds(h, tk, stride=n_heads), :].astype(BF16) for h in range(n_heads)]
    step(_qk(qbd_ref[...], k2d) + head_bias(ncc_ref[0]), vals)

    @pl.when(j == pl.num_programs(1) - 1)
    def _():
        s = _qk(qbd_ref[...], kn_ref[0]) + head_bias(ncn_ref[0])
        qry = lax.broadcasted_iota(jnp.int32, s.shape, 0) % tq
        s = jnp.where(lax.broadcasted_iota(jnp.int32, s.shape, 1) <= qry, s, NEG_INF)
        step(s, [vn_ref[0, :, h * head_dim:(h + 1) * head_dim] for h in range(n_heads)])
        l = l_ref[...]
        for h in range(n_heads):
            o_ref[0, :, h * head_dim:(h + 1) * head_dim] = (
                acc_ref[h] / l[h * tq:(h + 1) * tq]).astype(o_ref.dtype)


def _sample_attn_call(q, cache_k, cache_v, negc_cache, k_new, v_new, negc_new, n_heads, head_dim):
    b, t, da = q.shape
    past = cache_k.shape[1] // n_heads
    tk = min(CACHE_BLOCK, past)
    assert past % tk == 0
    tok = pl.BlockSpec((1, t, da), lambda bi, j: (bi, 0, 0))
    cache = pl.BlockSpec((1, tk * n_heads, head_dim), lambda bi, j: (bi, j, 0))
    rows = n_heads * t
    vmem = 2 * 2 * tk * da * 4 + 3 * tk * da * 2 + 8 * t * da * 2 + 6 * rows * tk * 4 + rows * da * 2
    return pl.pallas_call(
        functools.partial(_sample_attn_kernel, n_heads=n_heads, head_dim=head_dim),
        grid=(b, past // tk),
        in_specs=[tok, cache, cache,
                  pl.BlockSpec((1, n_heads, tk), lambda bi, j: (bi, 0, j)),
                  tok, tok,
                  pl.BlockSpec((1, n_heads, t), lambda bi, j: (bi, 0, 0))],
        out_specs=tok,
        out_shape=jax.ShapeDtypeStruct(q.shape, BF16),
        scratch_shapes=[pltpu.VMEM((rows, da), BF16), pltpu.VMEM((rows, 1), F32),
                        pltpu.VMEM((rows, 1), F32), pltpu.VMEM((n_heads, t, head_dim), F32)],
        compiler_params=_params(("arbitrary", "arbitrary"), vmem),
        name="fox_attn_sample",
    )(q, cache_k, cache_v, negc_cache, k_new, v_new, negc_new)


def _cpow(e, ar2, ai2):
    mag = jnp.exp(e * ar2)
    ang = e * ai2
    return mag * jnp.cos(ang), mag * jnp.sin(ang)


def _s5_prep_kernel(ldt_ref, a_ref, bt_ref, c_ref, m1_ref, min_ref, moutt_ref, asc_ref, *, chunk):
    L = chunk
    p2 = a_ref.shape[2] // 2
    p = p2 // 2
    gsz = c_ref.shape[1]
    lc = L * gsz
    dt = jnp.exp(ldt_ref[0])
    a_re2 = a_ref[0, :, 0:p2]
    a_im2 = a_ref[0, :, p2:2 * p2]
    ar2 = a_re2 * dt
    ai2 = a_im2 * dt
    lane = lax.broadcasted_iota(jnp.int32, (1, p2), 1)
    sgn = jnp.where(lane < p, 1.0, -1.0).astype(F32)

    ab_re, ab_im = _cpow(jnp.ones((1, 1), F32), ar2, ai2)
    x, y = ab_re - 1.0, ab_im
    den = a_re2 * a_re2 + a_im2 * a_im2
    cf_re = (x * a_re2 + y * a_im2) / den
    cf_im = (y * a_re2 - x * a_im2) / den
    bt = bt_ref[0]
    bbar = cf_re * bt[:, 0:p2] + cf_im * bt[:, p2:2 * p2]
    bbar_sw = pltpu.roll(bbar, p, axis=1) * (-sgn)

    cc = c_ref[0]
    tile = lambda m: jnp.broadcast_to(m[None], (L,) + m.shape).reshape(L * m.shape[0], m.shape[1])
    c_a, c_b = tile(cc[:, 0:p2]), tile(cc[:, p2:2 * p2])
    assert gsz & (gsz - 1) == 0
    tok = lax.shift_right_logical(lax.broadcasted_iota(jnp.int32, (lc, 1), 0),
                                  int(math.log2(gsz))).astype(F32)

    pw_re, pw_im = _cpow(tok, ar2, ai2)
    et = pw_re * c_a + pw_im * c_b
    r0 = lax.dot_general(bbar * sgn, et, (((1,), (1,)), ((), ())),
                         precision=lax.Precision.HIGHEST, preferred_element_type=F32)
    col = lax.broadcasted_iota(jnp.int32, (gsz, lc), 1)
    for j in range(L):
        blk = r0 if j == 0 else jnp.where(col >= j * gsz, pltpu.roll(r0, j * gsz, axis=1), 0.0)
        m1_ref[0, j * gsz:(j + 1) * gsz, :] = blk.astype(m1_ref.dtype)

    pw_re, pw_im = _cpow((L - 1.0) - tok, ar2, ai2)
    min_ref[0] = (pw_re * tile(bbar) + pw_im * tile(bbar_sw)).astype(min_ref.dtype)

    pw_re, pw_im = _cpow(tok + 1.0, ar2, ai2)
    moutt_ref[0] = ((pw_re * c_a + pw_im * c_b) * sgn).astype(moutt_ref.dtype)

    kk = lax.broadcasted_iota(jnp.int32, (8, 1), 0)
    e = (L * jnp.left_shift(1, kk)).astype(F32)
    pw_re, pw_im = _cpow(e, ar2, ai2)
    asc_ref[0, 0:8, :] = pw_re
    asc_ref[0, 8:16, :] = -pw_im * sgn


def _s5_prep_call(log_dt, a_re, a_im, b_re, b_im, c_re, c_im, chunk):
    g, p = a_re.shape
    gsz = c_re.shape[1]
    p2 = 2 * p
    lc = chunk * gsz
    ldt = jnp.broadcast_to(log_dt.reshape(g, 1, 1), (g, 1, p2))
    a_pack = jnp.concatenate([a_re, a_re, a_im, a_im], axis=-1).reshape(g, 1, 2 * p2)
    bt_re, bt_im = jnp.swapaxes(b_re, 1, 2), jnp.swapaxes(b_im, 1, 2)
    bt_pack = jnp.concatenate([bt_re, bt_im, -bt_im, bt_re], axis=-1)
    c_pack = jnp.concatenate([c_re, c_im, -c_im, c_re], axis=-1)
    grp = lambda shape: pl.BlockSpec((1,) + shape, lambda i: (i, 0, 0))
    return pl.pallas_call(
        functools.partial(_s5_prep_kernel, chunk=chunk),
        grid=(g,),
        in_specs=[grp((1, p2)), grp((1, 2 * p2)), grp((gsz, 2 * p2)), grp((gsz, 2 * p2))],
        out_specs=(grp((lc, lc)), grp((lc, p2)), grp((lc, p2)), grp((16, p2))),
        out_shape=(jax.ShapeDtypeStruct((g, lc, lc), BF16),
                   jax.ShapeDtypeStruct((g, lc, p2), BF16),
                   jax.ShapeDtypeStruct((g, lc, p2), BF16),
                   jax.ShapeDtypeStruct((g, 16, p2), F32)),
        compiler_params=_params(("arbitrary",), 0),
        name="s5_prep",
    )(ldt, a_pack, bt_pack, c_pack)


def _gelu_tanh(y):
    return 0.5 * y * (1.0 + jnp.tanh(math.sqrt(2.0 / math.pi) * (y + 0.044715 * (y * y * y))))


def _s5_kernel(*refs, n_chunks, has_h0):
    if has_h0:
        x_ref, m1_ref, min_ref, moutt_ref, asc_ref, d_ref, h0_ref, y_ref, hl_ref = refs
    else:
        x_ref, m1_ref, min_ref, moutt_ref, asc_ref, d_ref, y_ref, hl_ref = refs
    x = x_ref[0]
    rows = x.shape[0]
    p2 = min_ref.shape[2]
    nb = rows // n_chunks
    swap = lambda z: pltpu.roll(z, p2 // 2, axis=1)
    e = jnp.dot(x, min_ref[0], preferred_element_type=F32)
    if has_h0:
        h_in = h0_ref[0]
        e = e + asc_ref[0, 0:1, :] * h_in + asc_ref[0, 8:9, :] * swap(h_in)
    else:
        cidx = lax.broadcasted_iota(jnp.int32, (rows, 1), 0) & (n_chunks - 1)
        for k in range(int(math.log2(n_chunks))):
            sh = 1 << k
            prev = jnp.where(cidx >= sh, pltpu.roll(e, sh, axis=0), 0.0)
            e = e + asc_ref[0, k:k + 1, :] * prev + asc_ref[0, 8 + k:9 + k, :] * swap(prev)
        h_in = jnp.where(cidx >= 1, pltpu.roll(e, 1, axis=0), 0.0)
    y = jnp.dot(x, m1_ref[0], preferred_element_type=F32)
    y = y + lax.dot_general(h_in.astype(BF16), moutt_ref[0], (((1,), (1,)), ((), ())),
                            preferred_element_type=F32)
    y = y + d_ref[0] * x.astype(F32)
    y_ref[0] = _gelu_tanh(y)
    for b in range(nb):
        hl_ref[0, b:b + 1, :] = e[(b + 1) * n_chunks - 1:(b + 1) * n_chunks, :]


def _s5_call(xg, mats, d_vec, h0, n_chunks):
    m1, mn, moutt, asc = mats
    g, rows, lc = xg.shape
    p2 = mn.shape[2]
    nb = rows // n_chunks
    assert n_chunks & (n_chunks - 1) == 0 and n_chunks <= 256
    assert h0 is None or n_chunks == 1
    grp = lambda shape: pl.BlockSpec((1,) + shape, lambda i: (i, 0, 0))
    in_specs = [grp((rows, lc)), grp((lc, lc)), grp((lc, p2)), grp((lc, p2)), grp((16, p2)),
                grp((1, lc))]
    args = [xg, m1, mn, moutt, asc, d_vec]
    if h0 is not None:
        in_specs.append(grp((nb, p2)))
        args.append(h0)
    vmem = 2 * rows * lc * (2 + 4) + 8 * rows * lc * 4 + 4 * lc * lc * 2
    return pl.pallas_call(
        functools.partial(_s5_kernel, n_chunks=n_chunks, has_h0=h0 is not None),
        grid=(g,),
        in_specs=in_specs,
        out_specs=(grp((rows, lc)), grp((nb, p2))),
        out_shape=(jax.ShapeDtypeStruct((g, rows, lc), F32),
                   jax.ShapeDtypeStruct((g, nb, p2), F32)),
        compiler_params=_params(("arbitrary",), vmem),
        name="s5_chunked",
    )(*args)


def _sigmoid(z):
    return 1.0 / (1.0 + jnp.exp(-z))


def _out_kernel(x_ref, att_ref, ga_ref, s_ref, gs_ref, pe_ref,
                wglu_ref, wout_ref, wpe_ref, wpg_ref, gpe_ref, gfin_ref, y_ref, *, d_att):
    s = s_ref[...]
    z = jnp.dot(s.astype(BF16), wglu_ref[...], preferred_element_type=F32)
    ssm = s * _sigmoid(z)
    gs = gs_ref[...].astype(F32)
    m_s = (ssm * (gs * _sigmoid(gs))).astype(BF16)
    ga = ga_ref[...].astype(F32)
    m_a = (att_ref[...].astype(F32) * (ga * _sigmoid(ga))).astype(BF16)
    h = x_ref[...] + jnp.dot(m_a, wout_ref[0:d_att, :], preferred_element_type=F32)
    h = h + jnp.dot(m_s, wout_ref[d_att:, :], preferred_element_type=F32)
    e = jnp.dot(pe_ref[...].astype(BF16), wpe_ref[...], preferred_element_type=F32)
    e = e * lax.rsqrt(jnp.mean(e * e, axis=-1, keepdims=True) + EPS) * gpe_ref[...]
    gate = _sigmoid(jnp.dot(h.astype(BF16), wpg_ref[...], preferred_element_type=F32))
    h = h + e * gate
    y_ref[...] = h * lax.rsqrt(jnp.mean(h * h, axis=-1, keepdims=True) + EPS) * gfin_ref[...]


def _out_call(x2, att, ga, s, gs, pe, w_glu, w_out, w_pe, w_pg, g_pe, g_fin):
    n_tok, d = x2.shape
    d_att, d_ssm, d_ple = att.shape[1], s.shape[1], pe.shape[1]
    tm = _row_tile(n_tok, 256)
    row = lambda i: (i, 0)
    vmem = ((d_ssm * d_ssm + (d_att + d_ssm) * d + d_ple * d + d * d) * 2
            + 2 * tm * (2 * d * 4 + 2 * d_att * 2 + d_ssm * 4 + d_ssm * 2 + d_ple * 4)
            + 6 * tm * d * 4)
    return pl.pallas_call(
        functools.partial(_out_kernel, d_att=d_att),
        grid=(n_tok // tm,),
        in_specs=[pl.BlockSpec((tm, d), row), pl.BlockSpec((tm, d_att), row),
                  pl.BlockSpec((tm, d_att), row), pl.BlockSpec((tm, d_ssm), row),
                  pl.BlockSpec((tm, d_ssm), row), pl.BlockSpec((tm, d_ple), row),
                  _resident((d_ssm, d_ssm)), _resident((d_att + d_ssm, d)),
                  _resident((d_ple, d)), _resident((d, d)), _resident((1, d)), _resident((1, d))],
        out_specs=pl.BlockSpec((tm, d), row),
        out_shape=jax.ShapeDtypeStruct((n_tok, d), F32),
        compiler_params=_params(("arbitrary",), vmem),
        name="out_proj_ple_norm",
    )(x2, att, ga, s, gs, pe, w_glu, w_out, w_pe, w_pg, g_pe, g_fin)


def _pad_lanes(x, mult):
    pad = (-x.shape[-1]) % mult
    return x if pad == 0 else jnp.pad(x, [(0, 0)] * (x.ndim - 1) + [(0, pad)])


def _trunk(x, pe, past, wts, s5_mats, dims):
    n_heads, head_dim, n_groups, gsz, state = dims
    b, t, d = x.shape
    d_att, d_ssm = n_heads * head_dim, n_groups * gsz
    n_tok = b * t
    x2 = x.reshape(n_tok, d)
    n2, q, k32, v32, kb, vb, lf = _qkv_call(x2, wts["g_in"], wts["w_qkv"], wts["w_f"], wts["b_f"],
                                            n_heads, head_dim, transposed=past is None)
    ga, u, gs = _gate_call(n2, wts["w_gug"], d_att, d_ssm)
    logf = lf[:, :n_heads].reshape(b, t, n_heads)
    lf_rows = jnp.swapaxes(logf, 1, 2)
    seq = lambda a: a.reshape(b, t, d_att)

    if past is None:
        cx = _bias_cols_call(lf.reshape(b, t, LANES_V7X), n_heads)
        att = _prompt_attn_call(q, seq(kb), cx, vb, b, n_heads, head_dim)
        h0 = None
    else:
        cache_k, cache_v, cache_logf, h0_re, h0_im = past
        n_past = cache_k.shape[2]
        lf_all = jnp.concatenate([jnp.swapaxes(cache_logf[0], 1, 2), lf_rows], axis=2)
        lf_all = _pad_lanes(lf_all, CUMSUM_CHUNK)
        negc = _negcumsum_call(lf_all.reshape(b * n_heads, -1)).reshape(b, n_heads, -1)
        flat = lambda c: c.reshape(b, n_past * n_heads, head_dim)
        att = _sample_attn_call(seq(q), flat(cache_k), flat(cache_v), negc[..., :n_past], seq(kb),
                                seq(vb), negc[..., n_past:n_past + t], n_heads, head_dim)
        h0 = jnp.swapaxes(jnp.concatenate([h0_re[0], h0_im[0]], axis=-1), 0, 1)

    L = S5_CHUNK
    n_chunks = t // L
    xg = u.reshape(b * n_chunks, L, n_groups, gsz).transpose(2, 0, 1, 3).reshape(
        n_groups, b * n_chunks, L * gsz)
    yg, hl = _s5_call(xg, s5_mats, wts["d_vec"], h0, n_chunks)
    s = yg.reshape(n_groups, b * n_chunks, L, gsz).transpose(1, 2, 0, 3).reshape(n_tok, d_ssm)
    hl = jnp.swapaxes(hl, 0, 1)

    y = _out_call(x2, att.reshape(n_tok, d_att), ga, s, gs, pe.reshape(n_tok, -1),
                  wts["w_glu"], wts["w_out"], wts["w_pe"], wts["w_pg"], wts["g_pe"], wts["g_final"])
    return (y.reshape(b, t, d),
            k32.reshape(1, b, t, n_heads, head_dim), v32.reshape(1, b, t, n_heads, head_dim),
            logf[None], hl[None, :, :, :state], hl[None, :, :, state:])


def kernel(x_prompt, x_sample, p_prompt, p_sample, cache_k, cache_v, cache_logf, state_ssm_re, state_ssm_im, g_in, w_in, b_f, a_re, a_im, log_dt, b_re, b_im, c_re, c_im, d_skip, w_glu, w_out, w_pe, g_pe, w_pg, g_final):
    assert g_in.shape[0] == 1, "single-layer trunk"
    n_heads, head_dim = cache_k.shape[3], cache_k.shape[4]
    n_groups, state = a_re.shape[1], a_re.shape[2]
    gsz = b_re.shape[3]
    d = x_prompt.shape[-1]
    d_att, d_ssm = n_heads * head_dim, n_groups * gsz
    assert 2 * state == LANES_V7X and head_dim == LANES_V7X
    dims = (n_heads, head_dim, n_groups, gsz, state)

    off_f = 3 * d_att
    off_ga = off_f + n_heads
    w = w_in[0]
    wts = {
        "g_in": g_in[0].reshape(1, d),
        "w_qkv": w[:, :off_f].astype(BF16),
        "w_f": _pad_lanes(w[:, off_f:off_ga], LANES_V7X).astype(BF16),
        "b_f": _pad_lanes(b_f[0].reshape(1, n_heads), LANES_V7X),
        "w_gug": w[:, off_ga:].astype(BF16),
        "d_vec": jnp.tile(d_skip[0].reshape(n_groups, 1, gsz), (1, 1, S5_CHUNK)),
        "w_glu": w_glu[0].astype(BF16),
        "w_out": w_out[0].astype(BF16),
        "w_pe": w_pe[0].astype(BF16),
        "w_pg": w_pg[0].astype(BF16),
        "g_pe": g_pe[0].reshape(1, d),
        "g_final": g_final.reshape(1, d),
    }
    s5_mats = _s5_prep_call(log_dt[0], a_re[0], a_im[0], b_re[0], b_im[0], c_re[0], c_im[0], S5_CHUNK)

    out_p = _trunk(x_prompt, p_prompt[0], None, wts, s5_mats, dims)
    out_s = _trunk(x_sample, p_sample[0],
                   (cache_k, cache_v, cache_logf, state_ssm_re, state_ssm_im), wts, s5_mats, dims)
    return (out_p[0], out_s[0]) + out_p[1:] + out_s[1:]
```
